```python
import math
import jax, jax.numpy as jnp
from jax import lax
import numpy as np

D_MODEL = 2048
BATCH = 4
SEQ = 4096
DEPTH = 1

CHUNK = 64
EPS = 1e-6
FOX_HEADS = 8
FOX_HEAD_DIM = 128
FOX_WIDTH = FOX_HEADS * FOX_HEAD_DIM
Q_BLOCK = 128
HGRN_HEADS = 8
HGRN_DK = 128
HGRN_DV = 128
HGRN_KW = HGRN_HEADS * HGRN_DK
HGRN_VW = HGRN_HEADS * HGRN_DV
COL_SIZES = (FOX_WIDTH, FOX_WIDTH, FOX_WIDTH, FOX_HEADS,
             HGRN_KW, HGRN_KW, HGRN_VW, HGRN_VW,
             D_MODEL, D_MODEL)
N_COLS = sum(COL_SIZES)
N_GROUPS = 4
EXPERTS_PER_GROUP = 8
N_EXPERTS = N_GROUPS * EXPERTS_PER_GROUP
TOP_K = 2
D_FF_EXPERT = 1024
MOE_BLOCK = 256

kernel_name = "fox_hgrn2_gated_hier_moe_block"


def rms_norm(x, g):
    x32 = x.astype(jnp.float32)
    y = x32 * lax.rsqrt(jnp.mean(x32 * x32, axis=-1, keepdims=True) + EPS)
    return (y * g.astype(jnp.float32)).astype(x.dtype)


def forgetting_attention(q, k, v, log_f):
    seq = q.shape[2]
    c = jnp.cumsum(log_f, axis=-1)
    scale = FOX_HEAD_DIM ** -0.5
    outs = []
    for blk in range(seq // Q_BLOCK):
        lo, hi = blk * Q_BLOCK, (blk + 1) * Q_BLOCK
        s = jnp.einsum('bhqd,bhkd->bhqk', q[:, :, lo:hi], k[:, :, :hi]).astype(jnp.float32) * scale
        s = s + c[:, :, lo:hi, None] - c[:, :, None, :hi]
        mask = jnp.arange(hi)[None, :] <= jnp.arange(lo, hi)[:, None]
        p = jax.nn.softmax(jnp.where(mask, s, -jnp.inf), axis=-1)
        outs.append(jnp.einsum('bhqk,bhkd->bhqd', p.astype(v.dtype), v[:, :, :hi]))
    return jnp.concatenate(outs, axis=2)


def hgrn2_chunkwise(q, k, v, log_f):
    b_, h_, seq, dk = q.shape
    dv = v.shape[-1]
    nc = seq // CHUNK

    def to_chunks(t):
        return t.reshape(b_, h_, nc, CHUNK, t.shape[-1]).transpose(2, 0, 1, 3, 4)

    causal = jnp.tril(jnp.ones((CHUNK, CHUNK), dtype=bool))[:, :, None]

    def step(state, inp):
        qc, kc, vc, lfc = inp
        b = jnp.cumsum(lfc, axis=2)
        inter = jnp.einsum('bhtk,bhkv->bhtv', qc * jnp.exp(b), state)
        diff = b[:, :, :, None, :] - b[:, :, None, :, :]
        decay = jnp.exp(jnp.where(causal, diff, -jnp.inf))
        scores = jnp.einsum('bhtsk,bhsk->bhts', qc[:, :, :, None, :] * decay, kc)
        intra = jnp.einsum('bhts,bhsv->bhtv', scores, vc)
        b_last = b[:, :, -1:, :]
        new_state = jnp.exp(b_last[:, :, 0, :, None]) * state + \
            jnp.einsum('bhsk,bhsv->bhkv', kc * jnp.exp(b_last - b), vc)
        return new_state, inter + intra

    state0 = jnp.zeros((b_, h_, dk, dv), jnp.float32)
    _, out = lax.scan(step, state0, (to_chunks(q), to_chunks(k), to_chunks(v), to_chunks(log_f)))
    return out.transpose(1, 2, 0, 3, 4).reshape(b_, h_, seq, dv)


def expert_ffn(xb, w1, w3, w2):
    return (jax.nn.silu(xb @ w1) * (xb @ w3)) @ w2


def hier_moe(h, w_group, b_group, w_expert, b_expert, w1, w3, w2):
    t_tok, d = h.shape
    pg = jax.nn.softmax((h @ w_group + b_group).astype(jnp.float32), axis=-1)
    g_w, g_idx = lax.top_k(pg, 1)
    le = (h @ w_expert + b_expert).astype(jnp.float32).reshape(t_tok, N_GROUPS, EXPERTS_PER_GROUP)
    le_g = jnp.einsum('tge,tg->te', le, jax.nn.one_hot(g_idx[:, 0], N_GROUPS, dtype=jnp.float32))
    pe = jax.nn.softmax(le_g, axis=-1)
    top_w, top_i = lax.top_k(pe, TOP_K)
    top_w = top_w / jnp.sum(top_w, axis=-1, keepdims=True)
    gate = (g_w * top_w).reshape(-1)
    e_flat = (g_idx * EXPERTS_PER_GROUP + top_i).reshape(-1).astype(jnp.int32)
    tok_flat = jnp.repeat(jnp.arange(t_tok, dtype=jnp.int32), TOP_K)
    n_assign = t_tok * TOP_K

    order = jnp.argsort(e_flat)
    e_s, tok_s, w_s = e_flat[order], tok_flat[order], gate[order]
    counts = jnp.bincount(e_flat, length=N_EXPERTS)
    padded = ((counts + MOE_BLOCK - 1) // MOE_BLOCK) * MOE_BLOCK
    pad_end = jnp.cumsum(padded)
    pad_start = pad_end - padded
    start = jnp.cumsum(counts) - counts
    dest = pad_start[e_s] + (jnp.arange(n_assign) - start[e_s])
    n_blocks = -(-n_assign // MOE_BLOCK) + N_EXPERTS
    x_buf = jnp.zeros((n_blocks * MOE_BLOCK, d), h.dtype).at[dest].set(h[tok_s])
    blk_start = jnp.arange(n_blocks) * MOE_BLOCK
    blk_expert = jnp.minimum(jnp.sum(pad_end[None, :] <= blk_start[:, None], axis=1), N_EXPERTS - 1)

    def run_block(args):
        xb, e = args
        return expert_ffn(xb, w1[e], w3[e], w2[e])

    y_buf = lax.map(run_block, (x_buf.reshape(n_blocks, MOE_BLOCK, d), blk_expert)).reshape(-1, d)
    y = y_buf[dest] * w_s[:, None].astype(h.dtype)
    return jax.ops.segment_sum(y, tok_s, num_segments=t_tok)


def setup_inputs(seed: int = 0) -> dict:
    key = jax.random.key(seed)
    ks = jax.random.split(key, 20)
    f32 = jnp.float32
    nrm = lambda k, shape, s: jax.random.normal(k, shape, f32) * s
    return {
        "x": jax.random.normal(ks[0], (BATCH, SEQ, D_MODEL), f32),
        "norm_mix_g": 1.0 + nrm(ks[1], (DEPTH, D_MODEL), 0.02),
        "w_in": nrm(ks[2], (DEPTH, D_MODEL, N_COLS), D_MODEL ** -0.5),
        "b_fgate": 2.0 + nrm(ks[3], (DEPTH, FOX_HEADS), 0.5),
        "hgrn_lb_logits": nrm(ks[4], (DEPTH + 1, HGRN_KW), 0.5),
        "hgrn_norm_g": 1.0 + nrm(ks[5], (DEPTH, HGRN_VW), 0.02),
        "w_branch_a": nrm(ks[6], (DEPTH, FOX_WIDTH, D_MODEL), FOX_WIDTH ** -0.5),
        "w_branch_b": nrm(ks[7], (DEPTH, HGRN_VW, D_MODEL), HGRN_VW ** -0.5),
        "w_out": nrm(ks[8], (DEPTH, D_MODEL, D_MODEL), D_MODEL ** -0.5),
        "norm_ffn_g": 1.0 + nrm(ks[9], (DEPTH, D_MODEL), 0.02),
        "w_group": nrm(ks[10], (DEPTH, D_MODEL, N_GROUPS), D_MODEL ** -0.5),
        "b_group": nrm(ks[11], (DEPTH, N_GROUPS), 0.01),
        "w_expert": nrm(ks[12], (DEPTH, D_MODEL, N_EXPERTS), D_MODEL ** -0.5),
        "b_expert": nrm(ks[13], (DEPTH, N_EXPERTS), 0.01),
        "moe_w1": nrm(ks[14], (DEPTH, N_EXPERTS, D_MODEL, D_FF_EXPERT), D_MODEL ** -0.5),
        "moe_w3": nrm(ks[15], (DEPTH, N_EXPERTS, D_MODEL, D_FF_EXPERT), D_MODEL ** -0.5),
        "moe_w2": nrm(ks[16], (DEPTH, N_EXPERTS, D_FF_EXPERT, D_MODEL), D_FF_EXPERT ** -0.5),
        "norm_final_g": 1.0 + nrm(ks[17], (D_MODEL,), 0.02),
    }


def reference(x, norm_mix_g, w_in, b_fgate, hgrn_lb_logits, hgrn_norm_g, w_branch_a, w_branch_b,
              w_out, norm_ffn_g, w_group, b_group, w_expert, b_expert, moe_w1, moe_w3, moe_w2,
              norm_final_g):
    bsz, seq, d = x.shape
    f32 = jnp.float32
    split_idx = list(np.cumsum(COL_SIZES)[:-1])
    lb_all = jnp.cumsum(jax.nn.softmax(hgrn_lb_logits.astype(f32), axis=0), axis=0)

    def heads(t, n, hd):
        return t.reshape(bsz, seq, n, hd).transpose(0, 2, 1, 3)

    for layer in range(DEPTH):
        h = rms_norm(x, norm_mix_g[layer])
        proj = h @ w_in[layer]
        qa, ka, va, fa, qb, fb, ib, gb, gate_a, gate_b = jnp.split(proj, split_idx, axis=-1)

        log_f_a = jax.nn.log_sigmoid((fa + b_fgate[layer]).astype(f32)).transpose(0, 2, 1)
        oa = forgetting_attention(heads(qa, FOX_HEADS, FOX_HEAD_DIM), heads(ka, FOX_HEADS, FOX_HEAD_DIM),
                                  heads(va, FOX_HEADS, FOX_HEAD_DIM), log_f_a)
        ya = oa.transpose(0, 2, 1, 3).reshape(bsz, seq, FOX_WIDTH) @ w_branch_a[layer]

        lb = lb_all[layer]
        f_b = lb + (1.0 - lb) * jax.nn.sigmoid(fb.astype(f32))
        ob = hgrn2_chunkwise(heads(qb.astype(f32), HGRN_HEADS, HGRN_DK),
                             heads(1.0 - f_b, HGRN_HEADS, HGRN_DK),
                             heads(ib.astype(f32), HGRN_HEADS, HGRN_DV),
                             heads(jnp.log(f_b), HGRN_HEADS, HGRN_DK))
        ob = rms_norm(ob.transpose(0, 2, 1, 3).astype(x.dtype),
                      hgrn_norm_g[layer].reshape(HGRN_HEADS, HGRN_DV)).reshape(bsz, seq, HGRN_VW)
        yb = (ob * jax.nn.silu(gb)) @ w_branch_b[layer]

        merged = jax.nn.sigmoid(gate_a) * ya + jax.nn.sigmoid(gate_b) * yb
        x = x + merged @ w_out[layer]

        h2 = rms_norm(x, norm_ffn_g[layer]).reshape(bsz * seq, d)
        x = x + hier_moe(h2, w_group[layer], b_group[layer], w_expert[layer], b_expert[layer],
                         moe_w1[layer], moe_w3[layer], moe_w2[layer]).reshape(bsz, seq, d)

    return rms_norm(x, norm_final_g)
```

```python
import functools

import jax
import jax.numpy as jnp
from jax import lax
from jax.experimental import pallas as pl
from jax.experimental.pallas import tpu as pltpu

F32 = jnp.float32
BF16 = jnp.bfloat16

EPS = 1e-6
D_MODEL = 2048
FOX_HEADS = 8
FOX_HEAD_DIM = 128
FOX_WIDTH = FOX_HEADS * FOX_HEAD_DIM
HGRN_HEADS = 8
HGRN_DK = 128
HGRN_DV = 128
HGRN_KW = HGRN_HEADS * HGRN_DK
HGRN_VW = HGRN_HEADS * HGRN_DV
N_GROUPS = 4
EXPERTS_PER_GROUP = 8
N_EXPERTS = N_GROUPS * EXPERTS_PER_GROUP
D_FF_EXPERT = 1024
MOE_BLOCK = 256

LANES = 128
NEG_BIG = -1e30

COL_GATE_A = 0
COL_GATE_B = 2048
COL_FOX_Q = 4096
COL_FOX_K = 5120
COL_FOX_V = 6144
COL_H_Q = 7168
COL_H_F = 8192
COL_H_I = 9216
COL_H_G = 10240
N_PROJ = 11264

VMEM_LIMIT = 52 * 1024 * 1024


def _cparams(sem):
    return pltpu.CompilerParams(dimension_semantics=sem, vmem_limit_bytes=VMEM_LIMIT)


def _split3(x):
    hi = x.astype(BF16)
    r1 = x - hi.astype(F32)
    mid = r1.astype(BF16)
    lo = (r1 - mid.astype(F32)).astype(BF16)
    return hi, mid, lo


def _dot(a, b):
    return jnp.dot(a, b, preferred_element_type=F32)


def _dot_nt(a, b):
    return lax.dot_general(a, b, (((1,), (1,)), ((), ())), preferred_element_type=F32)


def _dot_tn(a, b):
    return lax.dot_general(a, b, (((0,), (0,)), ((), ())), preferred_element_type=F32)


def _sigmoid(x):
    return 1.0 / (1.0 + jnp.exp(-x))


def _inproj_kernel(x_ref, g_ref, w_ref, wfa_ref, proj_ref, fa_ref, h_ref):
    @pl.when(pl.program_id(1) == 0)
    def _():
        x = x_ref[...]
        ms = jnp.mean(x * x, axis=-1, keepdims=True)
        h = (x * lax.rsqrt(ms + EPS) * g_ref[...]).astype(BF16)
        h_ref[...] = h
        fa_ref[...] = _dot(h, wfa_ref[...])

    proj_ref[...] = _dot(h_ref[...], w_ref[...]).astype(BF16)


def _in_proj(x2, g, w_main, w_fa, tm, tn):
    t = x2.shape[0]
    return pl.pallas_call(
        _inproj_kernel,
        grid=(t // tm, N_PROJ // tn),
        in_specs=[
            pl.BlockSpec((tm, D_MODEL), lambda i, j: (i, 0)),
            pl.BlockSpec((1, D_MODEL), lambda i, j: (0, 0)),
            pl.BlockSpec((D_MODEL, tn), lambda i, j: (0, j)),
            pl.BlockSpec((D_MODEL, LANES), lambda i, j: (0, 0)),
        ],
        out_specs=[
            pl.BlockSpec((tm, tn), lambda i, j: (i, j)),
            pl.BlockSpec((tm, LANES), lambda i, j: (i, 0)),
        ],
        out_shape=[
            jax.ShapeDtypeStruct((t, N_PROJ), BF16),
            jax.ShapeDtypeStruct((t, LANES), F32),
        ],
        scratch_shapes=[pltpu.VMEM((tm, D_MODEL), BF16)],
        compiler_params=_cparams(("arbitrary", "arbitrary")),
        name="in_proj",
    )(x2, g, w_main, w_fa)


def _foxcum_kernel(fa_ref, bias_ref, c_ref, carry_ref, *, tc):
    @pl.when(pl.program_id(1) == 0)
    def _():
        carry_ref[...] = jnp.zeros_like(carry_ref)

    z = fa_ref[...] + bias_ref[...]
    lf = jnp.minimum(z, 0.0) - jnp.log(1.0 + jnp.exp(-jnp.abs(z)))
    hi, mid, lo = _split3(lf)
    row = lax.broadcasted_iota(jnp.int32, (tc, tc), 0)
    col = lax.broadcasted_iota(jnp.int32, (tc, tc), 1)
    tri = jnp.where(col <= row, 1.0, 0.0).astype(BF16)
    c = _dot(tri, hi) + _dot(tri, mid) + _dot(tri, lo) + carry_ref[...]
    carry_ref[...] = c[tc - 1:tc, :]
    c_ref[0] = c.T[0:FOX_HEADS, :]


def _fox_cum(fa, bias, bsz, seq, tc):
    return pl.pallas_call(
        functools.partial(_foxcum_kernel, tc=tc),
        grid=(bsz, seq // tc),
        in_specs=[
            pl.BlockSpec((tc, LANES), lambda b, s: (b * (seq // tc) + s, 0)),
            pl.BlockSpec((1, LANES), lambda b, s: (0, 0)),
        ],
        out_specs=pl.BlockSpec((1, FOX_HEADS, tc), lambda b, s: (b, 0, s)),
        out_shape=jax.ShapeDtypeStruct((bsz, FOX_HEADS, seq), F32),
        scratch_shapes=[pltpu.VMEM((1, LANES), F32)],
        compiler_params=_cparams(("arbitrary", "arbitrary")),
        name="fox_cum",
    )(fa, bias)


def _fox_kernel(q_ref, k_ref, v_ref, c_ref, o_ref, m_ref, l_ref, acc_ref, *, tq):
    h = pl.program_id(1)
    i = pl.program_id(2)
    q = q_ref[...]
    m_ref[...] = jnp.full_like(m_ref, NEG_BIG)
    l_ref[...] = jnp.zeros_like(l_ref)
    acc_ref[...] = jnp.zeros_like(acc_ref)

    def step(j, masked):
        start = pl.multiple_of(j * tq, tq)
        kj = k_ref[pl.ds(start, tq), :]
        vj = v_ref[pl.ds(start, tq), :]
        s = _dot_nt(q, kj) - c_ref[0, pl.ds(h, 1), pl.ds(start, tq)]
        if masked:
            row = lax.broadcasted_iota(jnp.int32, (tq, tq), 0)
            col = lax.broadcasted_iota(jnp.int32, (tq, tq), 1)
            s = jnp.where(col <= row, s, NEG_BIG)
        m_prev = m_ref[...]
        m_new = jnp.maximum(m_prev, jnp.max(s, axis=-1, keepdims=True))
        alpha = jnp.exp(m_prev - m_new)
        p = jnp.exp(s - m_new)
        l_ref[...] = alpha * l_ref[...] + jnp.sum(p, axis=-1, keepdims=True)
        acc_ref[...] = alpha * acc_ref[...] + _dot(p.astype(BF16), vj)
        m_ref[...] = m_new

    def body(j, carry):
        step(j, False)
        return carry

    lax.fori_loop(0, i, body, 0)
    step(i, True)
    o_ref[...] = (acc_ref[...] / l_ref[...]).astype(BF16)


def _fox_attn(proj, c, bsz, seq, tq):
    nq = seq // tq
    qb, kb, vb = COL_FOX_Q // LANES, COL_FOX_K // LANES, COL_FOX_V // LANES
    return pl.pallas_call(
        functools.partial(_fox_kernel, tq=tq),
        grid=(bsz, FOX_HEADS, nq),
        in_specs=[
            pl.BlockSpec((tq, LANES), lambda b, h, i: (b * nq + i, qb + h)),
            pl.BlockSpec((seq, LANES), lambda b, h, i: (b, kb + h)),
            pl.BlockSpec((seq, LANES), lambda b, h, i: (b, vb + h)),
            pl.BlockSpec((1, FOX_HEADS, seq), lambda b, h, i: (b, 0, 0)),
        ],
        out_specs=pl.BlockSpec((tq, LANES), lambda b, h, i: (b * nq + i, h)),
        out_shape=jax.ShapeDtypeStruct((bsz * seq, FOX_WIDTH), BF16),
        scratch_shapes=[
            pltpu.VMEM((tq, 1), F32),
            pltpu.VMEM((tq, 1), F32),
            pltpu.VMEM((tq, LANES), F32),
        ],
        compiler_params=_cparams(("arbitrary", "arbitrary", "arbitrary")),
        name="fox_attn",
    )(proj, proj, proj, c)


def _hgrn_kernel(q_ref, f_ref, i_ref, g_ref, lbl_ref, ng_ref, o_ref, st_ref, *, ts, chunk):
    @pl.when(pl.program_id(2) == 0)
    def _():
        st_ref[...] = jnp.zeros_like(st_ref)

    n_lvl = chunk.bit_length() - 1
    lbl = lbl_ref[...]
    mx = jnp.max(lbl, axis=0, keepdims=True)
    ex = jnp.exp(lbl - mx)
    lb = ex[0:1, :] / jnp.sum(ex, axis=0, keepdims=True)
    ng = ng_ref[...]

    row = lax.broadcasted_iota(jnp.int32, (chunk, chunk), 0)
    col = lax.broadcasted_iota(jnp.int32, (chunk, chunk), 1)
    xr = jnp.bitwise_xor(row, col)
    lower = col < row
    masks = [jnp.logical_and(jnp.right_shift(xr, k) == 1, lower) for k in range(n_lvl)]
    tri = jnp.where(col <= row, 1.0, 0.0).astype(BF16)
    trow = lax.broadcasted_iota(jnp.int32, (chunk, HGRN_DK), 0)

    for c in range(ts // chunk):
        rows = pl.ds(c * chunk, chunk)
        q = q_ref[rows, :].astype(F32)
        fb = f_ref[rows, :].astype(F32)
        v = i_ref[rows, :].astype(F32)
        gb = g_ref[rows, :].astype(F32)

        f = lb + (1.0 - lb) * _sigmoid(fb)
        lf = jnp.log(f)
        kk = 1.0 - f
        hi, mid, lo = _split3(lf)
        b = _dot(tri, hi) + _dot(tri, mid) + _dot(tri, lo)

        scores = jnp.where(masks[0], _dot_nt((q * jnp.exp(lf)).astype(BF16), kk.astype(BF16)), 0.0)
        g_end = b
        for k in range(1, n_lvl):
            m = 1 << k
            half = m >> 1
            shifted = pltpu.roll(g_end, chunk - half, axis=0)
            g_end = jnp.where(jnp.bitwise_and(trow, half) == 0, shifted, g_end)
            g_start = jnp.where(trow >= m, pltpu.roll(g_end, m, axis=0), 0.0)
            qm = (q * jnp.exp(b - g_start)).astype(BF16)
            km = (kk * jnp.exp(g_end - b)).astype(BF16)
            scores = jnp.where(masks[k], _dot_nt(qm, km), scores)

        diag = jnp.sum(q * kk, axis=-1, keepdims=True)
        v16 = v.astype(BF16)
        st = st_ref[...]
        inter = _dot_nt((q * jnp.exp(b)).astype(BF16), st.astype(BF16))
        o = inter + _dot(scores.astype(BF16), v16) + diag * v

        b_last = b[chunk - 1:chunk, :]
        khat = (kk * jnp.exp(b_last - b)).astype(BF16)
        st_ref[...] = st * jnp.exp(b_last) + _dot_tn(v16, khat)

        y = o * lax.rsqrt(jnp.mean(o * o, axis=-1, keepdims=True) + EPS) * ng
        o_ref[rows, :] = (y * (gb * _sigmoid(gb))).astype(BF16)


def _hgrn(proj, lb_logits, norm_g, bsz, seq, ts, chunk):
    ns = seq // ts
    qb, fb, ib, gb = COL_H_Q // LANES, COL_H_F // LANES, COL_H_I // LANES, COL_H_G // LANES
    nrow = lb_logits.shape[0]

    def col(base):
        return pl.BlockSpec((ts, LANES), lambda b, h, s: (b * ns + s, base + h))

    return pl.pallas_call(
        functools.partial(_hgrn_kernel, ts=ts, chunk=chunk),
        grid=(bsz, HGRN_HEADS, ns),
        in_specs=[
            col(qb), col(fb), col(ib), col(gb),
            pl.BlockSpec((nrow, LANES), lambda b, h, s: (0, h)),
            pl.BlockSpec((1, LANES), lambda b, h, s: (0, h)),
        ],
        out_specs=pl.BlockSpec((ts, LANES), lambda b, h, s: (b * ns + s, h)),
        out_shape=jax.ShapeDtypeStruct((bsz * seq, HGRN_VW), BF16),
        scratch_shapes=[pltpu.VMEM((HGRN_DV, HGRN_DK), F32)],
        compiler_params=_cparams(("arbitrary", "arbitrary", "arbitrary")),
        name="hgrn",
    )(proj, proj, proj, proj, lb_logits, norm_g)


def _merge_kernel(x_ref, oa_ref, ob_ref, ga_ref, gb_ref, wa_ref, wb_ref, wo_ref, g2_ref,
                  wrh_ref, wrl_ref, br_ref, x1_ref, h2_ref, lg_ref):
    ya = _dot(oa_ref[...], wa_ref[...])
    yb = _dot(ob_ref[...], wb_ref[...])
    merged = _sigmoid(ga_ref[...].astype(F32)) * ya + _sigmoid(gb_ref[...].astype(F32)) * yb
    x1 = x_ref[...] + _dot(merged.astype(BF16), wo_ref[...])
    x1_ref[...] = x1
    h2 = x1 * lax.rsqrt(jnp.mean(x1 * x1, axis=-1, keepdims=True) + EPS) * g2_ref[...]
    h2_ref[...] = h2
    hi = h2.astype(BF16)
    lo = (h2 - hi.astype(F32)).astype(BF16)
    wrh = wrh_ref[...]
    lg_ref[...] = _dot(hi, wrh) + _dot(lo, wrh) + _dot(hi, wrl_ref[...]) + br_ref[...]


def _merge(x2, oa, ob, proj, wa, wb, wo, g2, wrh, wrl, br, tm):
    t = x2.shape[0]
    const = lambda i: (0, 0)
    return pl.pallas_call(
        _merge_kernel,
        grid=(t // tm,),
        in_specs=[
            pl.BlockSpec((tm, D_MODEL), lambda i: (i, 0)),
            pl.BlockSpec((tm, FOX_WIDTH), lambda i: (i, 0)),
            pl.BlockSpec((tm, HGRN_VW), lambda i: (i, 0)),
            pl.BlockSpec((tm, D_MODEL), lambda i: (i, COL_GATE_A // D_MODEL)),
            pl.BlockSpec((tm, D_MODEL), lambda i: (i, COL_GATE_B // D_MODEL)),
            pl.BlockSpec((FOX_WIDTH, D_MODEL), const),
            pl.BlockSpec((HGRN_VW, D_MODEL), const),
            pl.BlockSpec((D_MODEL, D_MODEL), const),
            pl.BlockSpec((1, D_MODEL), const),
            pl.BlockSpec((D_MODEL, LANES), const),
            pl.BlockSpec((D_MODEL, LANES), const),
            pl.BlockSpec((1, LANES), const),
        ],
        out_specs=[
            pl.BlockSpec((tm, D_MODEL), lambda i: (i, 0)),
            pl.BlockSpec((tm, D_MODEL), lambda i: (i, 0)),
            pl.BlockSpec((tm, LANES), lambda i: (i, 0)),
        ],
        out_shape=[
            jax.ShapeDtypeStruct((t, D_MODEL), F32),
            jax.ShapeDtypeStruct((t, D_MODEL), F32),
            jax.ShapeDtypeStruct((t, LANES), F32),
        ],
        compiler_params=_cparams(("arbitrary",)),
        name="merge",
    )(x2, oa, ob, proj, proj, wa, wb, wo, g2, wrh, wrl, br)


def _route_kernel(lg_ref, gates_ref, info_ref, cnt_ref, carry_ref, *, tm):
    @pl.when(pl.program_id(0) == 0)
    def _():
        carry_ref[...] = jnp.zeros_like(carry_ref)

    lg = lg_ref[...]
    lane = lax.broadcasted_iota(jnp.int32, (tm, LANES), 1)
    lane_f = lane.astype(F32)
    big = float(LANES)

    gmask = lane < N_GROUPS
    gm = jnp.max(jnp.where(gmask, lg, -jnp.inf), axis=-1, keepdims=True)
    gsum = jnp.sum(jnp.where(gmask, jnp.exp(lg - gm), 0.0), axis=-1, keepdims=True)
    g_w = 1.0 / gsum
    gidx = jnp.min(jnp.where(jnp.logical_and(gmask, lg == gm), lane_f, big), axis=-1, keepdims=True)

    eid = lane - N_GROUPS
    egrp = jnp.right_shift(eid, 3).astype(F32)
    emask = jnp.logical_and(jnp.logical_and(lane >= N_GROUPS, lane < N_GROUPS + N_EXPERTS), egrp == gidx)
    el = jnp.where(emask, lg, -jnp.inf)
    m1 = jnp.max(el, axis=-1, keepdims=True)
    i1 = jnp.min(jnp.where(jnp.logical_and(emask, lg == m1), lane_f, big), axis=-1, keepdims=True)
    rest = jnp.logical_and(emask, lane_f != i1)
    m2 = jnp.max(jnp.where(rest, lg, -jnp.inf), axis=-1, keepdims=True)
    i2 = jnp.min(jnp.where(jnp.logical_and(rest, lg == m2), lane_f, big), axis=-1, keepdims=True)
    tt = jnp.exp(m2 - m1)
    w1 = g_w / (1.0 + tt)
    w2 = g_w * tt / (1.0 + tt)
    e1 = i1 - float(N_GROUPS)
    e2 = i2 - float(N_GROUPS)

    oh1 = lane_f == e1
    oh2 = lane_f == e2
    oh = jnp.where(jnp.logical_or(oh1, oh2), 1.0, 0.0)
    row = lax.broadcasted_iota(jnp.int32, (tm, tm), 0)
    col = lax.broadcasted_iota(jnp.int32, (tm, tm), 1)
    strict = jnp.where(col < row, 1.0, 0.0).astype(BF16)
    before = _dot(strict, oh.astype(BF16)) + carry_ref[...]
    r1 = jnp.sum(jnp.where(oh1, before, 0.0), axis=-1, keepdims=True)
    r2 = jnp.sum(jnp.where(oh2, before, 0.0), axis=-1, keepdims=True)
    total = carry_ref[...] + jnp.sum(oh, axis=0, keepdims=True)
    carry_ref[...] = total
    cnt_ref[...] = total

    gates_ref[...] = jnp.where(lane == 0, w1, jnp.where(lane == 1, w2, 0.0))
    info = jnp.where(lane == 0, e1, jnp.where(lane == 1, e2, jnp.where(lane == 2, r1, jnp.where(lane == 3, r2, 0.0))))
    info_ref[0] = info.T[0:8, :].astype(jnp.int32)


def _route(lg, tm):
    t = lg.shape[0]
    return pl.pallas_call(
        functools.partial(_route_kernel, tm=tm),
        grid=(t // tm,),
        in_specs=[pl.BlockSpec((tm, LANES), lambda i: (i, 0))],
        out_specs=[
            pl.BlockSpec((tm, LANES), lambda i: (i, 0)),
            pl.BlockSpec((1, 8, tm), lambda i: (i, 0, 0)),
            pl.BlockSpec((1, LANES), lambda i: (0, 0)),
        ],
        out_shape=[
            jax.ShapeDtypeStruct((t, LANES), F32),
            jax.ShapeDtypeStruct((t // tm, 8, tm), jnp.int32),
            jax.ShapeDtypeStruct((1, LANES), F32),
        ],
        scratch_shapes=[pltpu.VMEM((1, LANES), F32)],
        compiler_params=_cparams(("arbitrary",)),
        name="route",
    )(lg)


def _dispatch_kernel(bs_ref, cnt_ref, nb_ref, nu_ref, info_ref, h_ref, xbuf_ref, zero_ref, sem, zsem, bsem,
                     *, tm, n_blocks):
    step = pl.program_id(0)
    base = step * tm

    def row_copy(t, d):
        return pltpu.make_async_copy(h_ref.at[pl.ds(base + t, 1)], xbuf_ref.at[pl.ds(d, 1)], sem)

    def issue(t, carry):
        for slot in range(2):
            d = bs_ref[info_ref[0, slot, t]] * MOE_BLOCK + info_ref[0, 2 + slot, t]
            row_copy(t, d).start()
        return carry

    lax.fori_loop(0, tm, issue, 0)

    @pl.when(step == 0)
    def _():
        zero_ref[...] = jnp.zeros_like(zero_ref)

        def zero_row(r):
            return pltpu.make_async_copy(zero_ref.at[pl.ds(0, 1)], xbuf_ref.at[pl.ds(r, 1)], zsem)

        def zero_block(n):
            start = pl.multiple_of(n * MOE_BLOCK, MOE_BLOCK)
            return pltpu.make_async_copy(zero_ref, xbuf_ref.at[pl.ds(start, MOE_BLOCK)], bsem)

        def per_expert(e, carry):
            lo = bs_ref[e] * MOE_BLOCK + cnt_ref[e]
            hi = (bs_ref[e] + nb_ref[e]) * MOE_BLOCK

            def start_row(r, c):
                zero_row(r).start()
                return c

            def wait_row(r, c):
                zero_row(r).wait()
                return c

            lax.fori_loop(lo, hi, start_row, 0)
            lax.fori_loop(lo, hi, wait_row, 0)
            return carry

        lax.fori_loop(0, N_EXPERTS, per_expert, 0)

        def start_blk(n, c):
            zero_block(n).start()
            return c

        def wait_blk(n, c):
            zero_block(n).wait()
            return c

        lax.fori_loop(nu_ref[0], n_blocks, start_blk, 0)
        lax.fori_loop(nu_ref[0], n_blocks, wait_blk, 0)

    def drain(t, carry):
        row_copy(0, 0).wait()
        return carry

    lax.fori_loop(0, 2 * tm, drain, 0)


def _dispatch(blk_start, cnt, nblk, n_used, info, h2, n_blocks, tm):
    t = h2.shape[0]
    return pl.pallas_call(
        functools.partial(_dispatch_kernel, tm=tm, n_blocks=n_blocks),
        grid_spec=pltpu.PrefetchScalarGridSpec(
            num_scalar_prefetch=4,
            grid=(t // tm,),
            in_specs=[
                pl.BlockSpec((1, 8, tm), lambda i, *_: (i, 0, 0), memory_space=pltpu.SMEM),
                pl.BlockSpec(memory_space=pl.ANY),
            ],
            out_specs=pl.BlockSpec(memory_space=pl.ANY),
            scratch_shapes=[
                pltpu.VMEM((MOE_BLOCK, D_MODEL), F32),
                pltpu.SemaphoreType.DMA(()),
                pltpu.SemaphoreType.DMA(()),
                pltpu.SemaphoreType.DMA(()),
            ],
        ),
        out_shape=jax.ShapeDtypeStruct((n_blocks * MOE_BLOCK, D_MODEL), F32),
        compiler_params=_cparams(("arbitrary",)),
        name="dispatch",
    )(blk_start, cnt, nblk, n_used, info, h2)


def _experts_kernel(be_ref, nu_ref, x_ref, w1_ref, w3_ref, w2_ref, y_ref):
    n = pl.program_id(0)

    @pl.when(n < nu_ref[0])
    def _():
        x = x_ref[...].astype(BF16)
        h1 = _dot(x, w1_ref[0])
        h3 = _dot(x, w3_ref[0])
        gated = (h1 * _sigmoid(h1) * h3).astype(BF16)
        y_ref[...] = _dot(gated, w2_ref[0])

    @pl.when(n >= nu_ref[0])
    def _():
        y_ref[...] = jnp.zeros_like(y_ref)


def _experts(blk_expert, n_used, x_buf, w1, w3, w2):
    n_blocks = x_buf.shape[0] // MOE_BLOCK

    def used(n, nu):
        return jnp.minimum(n, nu[0] - 1)

    def wspec(shape):
        return pl.BlockSpec((1,) + shape, lambda n, be, nu: (be[used(n, nu)], 0, 0))

    return pl.pallas_call(
        _experts_kernel,
        grid_spec=pltpu.PrefetchScalarGridSpec(
            num_scalar_prefetch=2,
            grid=(n_blocks,),
            in_specs=[
                pl.BlockSpec((MOE_BLOCK, D_MODEL), lambda n, be, nu: (used(n, nu), 0)),
                wspec((D_MODEL, D_FF_EXPERT)),
                wspec((D_MODEL, D_FF_EXPERT)),
                wspec((D_FF_EXPERT, D_MODEL)),
            ],
            out_specs=pl.BlockSpec((MOE_BLOCK, D_MODEL), lambda n, be, nu: (n, 0)),
        ),
        out_shape=jax.ShapeDtypeStruct(x_buf.shape, F32),
        compiler_params=_cparams(("arbitrary",)),
        name="experts",
    )(blk_expert, n_used, x_buf, w1, w3, w2)


def _combine_kernel(bs_ref, info_ref, ybuf_ref, x1_ref, gates_ref, g_ref, o_ref, y1_ref, y2_ref, sem, *, tm):
    def row_copy(d, dst, t):
        return pltpu.make_async_copy(ybuf_ref.at[pl.ds(d, 1)], dst.at[pl.ds(t, 1)], sem)

    def issue(t, carry):
        for slot, dst in ((0, y1_ref), (1, y2_ref)):
            d = bs_ref[info_ref[0, slot, t]] * MOE_BLOCK + info_ref[0, 2 + slot, t]
            row_copy(d, dst, t).start()
        return carry

    lax.fori_loop(0, tm, issue, 0)

    def drain(t, carry):
        row_copy(0, y1_ref, 0).wait()
        return carry

    lax.fori_loop(0, 2 * tm, drain, 0)

    gates = gates_ref[...]
    x = x1_ref[...] + gates[:, 0:1] * y1_ref[...] + gates[:, 1:2] * y2_ref[...]
    o_ref[...] = x * lax.rsqrt(jnp.mean(x * x, axis=-1, keepdims=True) + EPS) * g_ref[...]


def _combine(blk_start, info, y_buf, x1, gates, g, tm):
    t = x1.shape[0]
    return pl.pallas_call(
        functools.partial(_combine_kernel, tm=tm),
        grid_spec=pltpu.PrefetchScalarGridSpec(
            num_scalar_prefetch=1,
            grid=(t // tm,),
            in_specs=[
                pl.BlockSpec((1, 8, tm), lambda i, bs: (i, 0, 0), memory_space=pltpu.SMEM),
                pl.BlockSpec(memory_space=pl.ANY),
                pl.BlockSpec((tm, D_MODEL), lambda i, bs: (i, 0)),
                pl.BlockSpec((tm, LANES), lambda i, bs: (i, 0)),
                pl.BlockSpec((1, D_MODEL), lambda i, bs: (0, 0)),
            ],
            out_specs=pl.BlockSpec((tm, D_MODEL), lambda i, bs: (i, 0)),
            scratch_shapes=[
                pltpu.VMEM((tm, D_MODEL), F32),
                pltpu.VMEM((tm, D_MODEL), F32),
                pltpu.SemaphoreType.DMA(()),
            ],
        ),
        out_shape=jax.ShapeDtypeStruct((t, D_MODEL), F32),
        compiler_params=_cparams(("arbitrary",)),
        name="combine",
    )(blk_start, info, y_buf, x1, gates, g)


def _pad_lanes(a):
    return jnp.pad(a, ((0, 0), (0, LANES - a.shape[1])))


def kernel(x, norm_mix_g, w_in, b_fgate, hgrn_lb_logits, hgrn_norm_g, w_branch_a, w_branch_b, w_out,
           norm_ffn_g, w_group, b_group, w_expert, b_expert, moe_w1, moe_w3, moe_w2, norm_final_g):
    bsz, seq, d = x.shape
    assert d == D_MODEL and norm_mix_g.shape[0] == 1
    t = bsz * seq
    x2 = x.reshape(t, d)

    w = w_in[0]
    o = 0
    pieces = []
    for size in (FOX_WIDTH, FOX_WIDTH, FOX_WIDTH, FOX_HEADS, HGRN_KW, HGRN_KW, HGRN_VW, HGRN_VW, D_MODEL, D_MODEL):
        pieces.append(w[:, o:o + size])
        o += size
    qa, ka, va, fa_w, qb, fb, ib, gb, gate_a, gate_b = pieces
    w_main = jnp.concatenate([gate_a, gate_b, qa * (FOX_HEAD_DIM ** -0.5), ka, va, qb, fb, ib, gb], axis=1).astype(BF16)
    w_fa = _pad_lanes(fa_w).astype(BF16)
    fa_bias = _pad_lanes(b_fgate[0][None, :].astype(F32))

    tm_in = min(1024, t)
    proj, fa = _in_proj(x2, norm_mix_g[0][None, :], w_main, w_fa, tm_in, 1024)

    c = _fox_cum(fa, fa_bias, bsz, seq, min(256, seq))
    oa = _fox_attn(proj, c, bsz, seq, min(512, seq))
    ob = _hgrn(proj, hgrn_lb_logits.astype(F32), hgrn_norm_g.astype(F32), bsz, seq, min(512, seq), 128)

    w_r = _pad_lanes(jnp.concatenate([w_group[0], w_expert[0]], axis=1).astype(F32))
    w_r_hi = w_r.astype(BF16)
    w_r_lo = (w_r - w_r_hi.astype(F32)).astype(BF16)
    b_r = _pad_lanes(jnp.concatenate([b_group[0], b_expert[0]])[None, :].astype(F32))
    x1, h2, logits = _merge(x2, oa, ob, proj, w_branch_a[0].astype(BF16), w_branch_b[0].astype(BF16),
                            w_out[0].astype(BF16), norm_ffn_g[0][None, :], w_r_hi, w_r_lo, b_r, 256)

    tm_r = min(256, t)
    gates, info, counts = _route(logits, tm_r)

    cnt = counts[0, :N_EXPERTS].astype(jnp.int32)
    nblk = (cnt + MOE_BLOCK - 1) // MOE_BLOCK
    blk_end = jnp.cumsum(nblk)
    blk_start = (blk_end - nblk).astype(jnp.int32)
    n_blocks = -(-(t * 2) // MOE_BLOCK) + N_EXPERTS
    blk_ids = jnp.arange(n_blocks, dtype=jnp.int32)
    blk_expert = jnp.minimum(jnp.sum(blk_end[None, :] <= blk_ids[:, None], axis=1), N_EXPERTS - 1).astype(jnp.int32)
    n_used = blk_end[-1:].astype(jnp.int32)

    x_buf = _dispatch(blk_start, cnt, nblk.astype(jnp.int32), n_used, info, h2, n_blocks, tm_r)
    y_buf = _experts(blk_expert, n_used, x_buf,
                     moe_w1[0].astype(BF16), moe_w3[0].astype(BF16), moe_w2[0].astype(BF16))
    out = _combine(blk_start, info, y_buf, x1, gates, norm_final_g[None, :], tm_r)
    return out.reshape(bsz, seq, d)
```
